```python
import jax
import jax.numpy as jnp
from jax import lax
import numpy as np

D_MODEL = 2048
BATCH = 4
SEQ = 8192
DEPTH = 4

N_MIXERS = 4
EPS = 1e-6

SSM_D_INNER = 2 * D_MODEL
SSM_HEAD_DIM = 64
SSM_N_HEADS = SSM_D_INNER // SSM_HEAD_DIM
SSM_N_GROUPS = 8
SSM_HEADS_PER_GROUP = SSM_N_HEADS // SSM_N_GROUPS
SSM_D_STATE = 128
SSM_CONV = 4
SSM_CHUNK = 128
SSM_CONV_DIM = SSM_D_INNER + 2 * SSM_N_GROUPS * SSM_D_STATE
SSM_IN_DIM = SSM_D_INNER + SSM_CONV_DIM + SSM_N_HEADS

NSA_N_HEADS = 16
NSA_HEAD_DIM = D_MODEL // NSA_N_HEADS
NSA_KV_GROUPS = 4
NSA_HEADS_PER_GROUP = NSA_N_HEADS // NSA_KV_GROUPS
NSA_KV_DIM = NSA_KV_GROUPS * NSA_HEAD_DIM
NSA_CMP_STRIDE = 16
NSA_CMP_BLOCK = 2 * NSA_CMP_STRIDE
NSA_CMP_HIDDEN = 256
NSA_SEL_BLOCK = 64
NSA_TOP_N = 16
NSA_WINDOW = 512
NSA_Q_BLOCK = 32
NSA_IN_DIM = NSA_N_HEADS * NSA_HEAD_DIM + 6 * NSA_KV_DIM + 3 * NSA_N_HEADS
NSA_FORCE = 1e4

GDN_QK_HEADS = 16
GDN_V_HEADS = 32
GDN_HEAD_DIM = 128
GDN_KEY_DIM = GDN_QK_HEADS * GDN_HEAD_DIM
GDN_VAL_DIM = GDN_V_HEADS * GDN_HEAD_DIM
GDN_CONV = 4
GDN_CONV_DIM = 2 * GDN_KEY_DIM + GDN_VAL_DIM
GDN_CHUNK = 64
GDN_IN_DIM = GDN_CONV_DIM + GDN_VAL_DIM + 2 * GDN_V_HEADS

SGU_CHUNK = 128
SGU_GROUPS = 16
SGU_HALF = 2 * D_MODEL
SGU_GROUP_DIM = SGU_HALF // SGU_GROUPS

FFN_HIDDEN = 5632
FFN_CONV = 3

kernel_name = 'hybrid_interleaved_ssd_nsa_gdn_sgu_convffn'


def _split(x, sizes):
    return jnp.split(x, np.cumsum(sizes)[:-1].tolist(), axis=-1)


def rms_norm(x, w):
    xf = x.astype(jnp.float32)
    y = xf * lax.rsqrt(jnp.mean(xf * xf, axis=-1, keepdims=True) + EPS)
    return (y * w.astype(jnp.float32)).astype(x.dtype)


def l2_normalize(x):
    xf = x.astype(jnp.float32)
    return xf * lax.rsqrt(jnp.sum(xf * xf, axis=-1, keepdims=True) + EPS)


def causal_dwconv(x, w, b=None):
    width = w.shape[0]
    y = lax.conv_general_dilated(x, w[:, None, :].astype(x.dtype), window_strides=(1,),
                                 padding=[(width - 1, 0)], dimension_numbers=('NWC', 'WIO', 'NWC'),
                                 feature_group_count=x.shape[-1])
    return y if b is None else y + b.astype(x.dtype)


def masked_softmax(s, mask):
    s = jnp.where(mask, s.astype(jnp.float32), -1e30)
    p = jnp.where(mask, jnp.exp(s - jnp.max(s, axis=-1, keepdims=True)), 0.0)
    den = jnp.sum(p, axis=-1, keepdims=True)
    return p / jnp.where(den > 0, den, 1.0)


def ssd_chunked(x, dt, a, bm, cm):
    bsz, seq, g, hg, p = x.shape
    n = bm.shape[-1]
    L = SSM_CHUNK
    nc = seq // L

    def chunks(t):
        return jnp.moveaxis(t.reshape((bsz, nc, L) + t.shape[2:]), 1, 0)

    causal = jnp.tril(jnp.ones((L, L), dtype=bool))[None, :, :, None, None]

    def step(state, inp):
        xc, dtc, bc, cc = inp
        cum = jnp.cumsum(dtc * a, axis=1)
        decay = jnp.exp(jnp.where(causal, cum[:, :, None] - cum[:, None], -jnp.inf))
        w = jnp.einsum('bign,bjgn->bijg', cc, bc)[..., None] * decay * dtc[:, None]
        y = jnp.einsum('bijgh,bjghp->bighp', w, xc)
        y = y + jnp.einsum('bign,bghpn->bighp', cc, state) * jnp.exp(cum)[..., None]
        last = cum[:, -1]
        wts = jnp.exp(last[:, None] - cum) * dtc
        state = state * jnp.exp(last)[..., None, None] + jnp.einsum('bjgh,bjghp,bjgn->bghpn', wts, xc, bc)
        return state, y

    init = jnp.zeros((bsz, g, hg, p, n), jnp.float32)
    _, ys = lax.scan(step, init, (chunks(x), chunks(dt), chunks(bm), chunks(cm)))
    return jnp.moveaxis(ys, 0, 1).reshape(bsz, seq, g, hg, p)


def mamba2_mixer(h, w_in, conv_w, conv_b, dt_bias, a_log, d_skip, norm_w, w_out):
    bsz, seq, _ = h.shape
    g, hg, p, n = SSM_N_GROUPS, SSM_HEADS_PER_GROUP, SSM_HEAD_DIM, SSM_D_STATE
    f32 = jnp.float32
    z, xbc, dt = _split(h @ w_in, [SSM_D_INNER, SSM_CONV_DIM, SSM_N_HEADS])
    xbc = jax.nn.silu(causal_dwconv(xbc, conv_w, conv_b))
    xs, bm, cm = _split(xbc, [SSM_D_INNER, g * n, g * n])
    xs = xs.reshape(bsz, seq, g, hg, p).astype(f32)
    dt = jax.nn.softplus(dt.astype(f32) + dt_bias.astype(f32)).reshape(bsz, seq, g, hg)
    a = -jnp.exp(a_log.astype(f32)).reshape(g, hg)
    y = ssd_chunked(xs, dt, a, bm.reshape(bsz, seq, g, n).astype(f32), cm.reshape(bsz, seq, g, n).astype(f32))
    y = y + d_skip.astype(f32).reshape(g, hg, 1) * xs
    y = y.reshape(bsz, seq, SSM_D_INNER) * jax.nn.silu(z.astype(f32))
    yg = y.reshape(bsz, seq, g, SSM_D_INNER // g)
    yg = yg * lax.rsqrt(jnp.mean(yg * yg, axis=-1, keepdims=True) + EPS)
    y = yg.reshape(bsz, seq, SSM_D_INNER) * norm_w.astype(f32)
    return y.astype(h.dtype) @ w_out


def nsa_compress(t, pos, w1, w2):
    bsz, seq, g, d = t.shape
    sub = t.reshape(bsz, seq // NSA_CMP_STRIDE, NSA_CMP_STRIDE, g, d)
    blocks = jnp.concatenate([sub[:, :-1], sub[:, 1:]], axis=2) + pos[None, None, :, None, :].astype(t.dtype)
    flat = jnp.moveaxis(blocks, 3, 2).reshape(bsz, blocks.shape[1], g, NSA_CMP_BLOCK * d)
    return jax.nn.silu(flat @ w1) @ w2


def nsa_mixer(h, w_in, q_norm, k_norm, cmp_pos, cmp_k_w1, cmp_k_w2, cmp_v_w1, cmp_v_w2, w_out):
    bsz, seq, _ = h.shape
    G, HG, DH = NSA_KV_GROUPS, NSA_HEADS_PER_GROUP, NSA_HEAD_DIM
    scale = DH ** -0.5
    q, kc, vc, ks, vs, kw, vw, gates = _split(h @ w_in, [NSA_N_HEADS * DH] + [NSA_KV_DIM] * 6 + [3 * NSA_N_HEADS])

    def kv(t):
        return t.reshape(bsz, seq, G, DH)

    q = rms_norm(q.reshape(bsz, seq, G, HG, DH), q_norm)
    gates = jax.nn.sigmoid(gates.astype(jnp.float32)).reshape(bsz, seq, 3, G, HG)
    k_cmp = rms_norm(nsa_compress(kv(kc), cmp_pos, cmp_k_w1, cmp_k_w2), k_norm)
    v_cmp = nsa_compress(kv(vc), cmp_pos, cmp_v_w1, cmp_v_w2)
    n_cmp = k_cmp.shape[1]
    cmp_start = jnp.arange(n_cmp) * NSA_CMP_STRIDE
    cmp_end = cmp_start + NSA_CMP_BLOCK - 1
    n_sel = seq // NSA_SEL_BLOCK
    sel_start = jnp.arange(n_sel) * NSA_SEL_BLOCK
    cover = ((cmp_start[:, None] <= sel_start[None, :] + NSA_SEL_BLOCK - 1)
             & (cmp_end[:, None] >= sel_start[None, :])).astype(jnp.float32)
    top_n = min(NSA_TOP_N, n_sel)

    def blockify(t):
        t = jnp.transpose(t.reshape(bsz, n_sel, NSA_SEL_BLOCK, G, DH), (0, 3, 1, 2, 4))
        return t.reshape(bsz * G * n_sel, NSA_SEL_BLOCK, DH)

    k_sel = blockify(rms_norm(kv(ks), k_norm))
    v_sel = blockify(kv(vs))
    base = ((jnp.arange(bsz)[:, None] * G + jnp.arange(G)[None, :]) * n_sel)[:, :, None, None]
    pad = ((0, 0), (NSA_WINDOW, 0), (0, 0), (0, 0))
    k_win = jnp.pad(rms_norm(kv(kw), k_norm), pad)
    v_win = jnp.pad(kv(vw), pad)
    span = NSA_WINDOW + NSA_Q_BLOCK

    def query_block(i):
        q0 = i * NSA_Q_BLOCK
        t = q0 + jnp.arange(NSA_Q_BLOCK)
        qb = lax.dynamic_slice_in_dim(q, q0, NSA_Q_BLOCK, axis=1)
        gb = lax.dynamic_slice_in_dim(gates, q0, NSA_Q_BLOCK, axis=1)
        p_c = masked_softmax(jnp.einsum('btghd,bngd->bghtn', qb, k_cmp) * scale, cmp_end[None, :] <= t[:, None])
        o_c = jnp.einsum('bghtn,bngd->btghd', p_c, v_cmp)
        imp = jnp.einsum('bghtn,nm->bgtm', p_c, cover)
        blk = (t // NSA_SEL_BLOCK)[:, None]
        j = jnp.arange(n_sel)[None, :]
        forced = (j == 0) | (j == blk) | (j == blk - 1)
        valid = sel_start[None, :] <= t[:, None]
        score = jnp.where(valid, jnp.where(forced, NSA_FORCE, imp), -NSA_FORCE)
        _, idx = lax.top_k(score, top_n)
        m_sel = top_n * NSA_SEL_BLOCK
        kg = k_sel[base + idx].reshape(bsz, G, NSA_Q_BLOCK, m_sel, DH)
        vg = v_sel[base + idx].reshape(bsz, G, NSA_Q_BLOCK, m_sel, DH)
        pos = (idx[..., None] * NSA_SEL_BLOCK + jnp.arange(NSA_SEL_BLOCK)).reshape(bsz, G, NSA_Q_BLOCK, m_sel)
        p_s = masked_softmax(jnp.einsum('btghd,bgtmd->bghtm', qb, kg) * scale, (pos <= t[:, None])[:, :, None])
        o_s = jnp.einsum('bghtm,bgtmd->btghd', p_s, vg)
        kwb = lax.dynamic_slice_in_dim(k_win, q0, span, axis=1)
        vwb = lax.dynamic_slice_in_dim(v_win, q0, span, axis=1)
        kpos = q0 - NSA_WINDOW + jnp.arange(span)
        mask_w = ((kpos[None, :] <= t[:, None]) & (kpos[None, :] > t[:, None] - NSA_WINDOW)
                  & (kpos[None, :] >= 0))
        p_w = masked_softmax(jnp.einsum('btghd,bkgd->bghtk', qb, kwb) * scale, mask_w)
        o_w = jnp.einsum('bghtk,bkgd->btghd', p_w, vwb)
        o = gb[:, :, 0][..., None] * o_c + gb[:, :, 1][..., None] * o_s + gb[:, :, 2][..., None] * o_w
        return o.astype(h.dtype)

    out = lax.map(query_block, jnp.arange(seq // NSA_Q_BLOCK))
    out = jnp.moveaxis(out, 0, 1).reshape(bsz, seq, NSA_N_HEADS * DH)
    return out @ w_out


def gated_delta_chunked(q, k, v, beta, g):
    bsz, seq, nh, dk = q.shape
    dv = v.shape[-1]
    c = GDN_CHUNK
    nc = seq // c

    def chunks(t):
        return jnp.moveaxis(t.reshape((bsz, nc, c, nh) + t.shape[3:]), (1, 3), (0, 2))

    incl = jnp.tril(jnp.ones((c, c), dtype=bool))
    strict = jnp.tril(jnp.ones((c, c), dtype=bool), -1)
    eye = jnp.eye(c, dtype=jnp.float32)

    def step(state, inp):
        qc, kc, vc, bc, gc = inp
        cum = jnp.cumsum(gc, axis=-1)
        decay = jnp.exp(jnp.where(incl, cum[..., :, None] - cum[..., None, :], -jnp.inf))
        kb = kc * bc[..., None]
        a_mat = eye + jnp.where(strict, jnp.einsum('bhid,bhjd->bhij', kb, kc) * decay, 0.0)
        rhs = jnp.concatenate([vc * bc[..., None], kb * jnp.exp(cum)[..., None]], axis=-1)
        sol = lax.linalg.triangular_solve(a_mat, rhs, left_side=True, lower=True, unit_diagonal=True)
        u, w = sol[..., :dv], sol[..., dv:]
        v_new = u - jnp.einsum('bhck,bhkv->bhcv', w, state)
        attn = jnp.where(incl, jnp.einsum('bhid,bhjd->bhij', qc, kc) * decay, 0.0)
        out = (jnp.einsum('bhck,bhkv->bhcv', qc * jnp.exp(cum)[..., None], state)
               + jnp.einsum('bhij,bhjv->bhiv', attn, v_new))
        last = cum[..., -1:]
        state = state * jnp.exp(last)[..., None] + jnp.einsum('bhck,bhcv->bhkv', kc * jnp.exp(last - cum)[..., None], v_new)
        return state, out

    init = jnp.zeros((bsz, nh, dk, dv), jnp.float32)
    _, ys = lax.scan(step, init, (chunks(q), chunks(k), chunks(v), chunks(beta), chunks(g)))
    return jnp.moveaxis(ys, (0, 2), (1, 3)).reshape(bsz, seq, nh, dv)


def gdn_mixer(h, w_in, conv_w, a_log, dt_bias, norm_w, w_out):
    bsz, seq, _ = h.shape
    f32 = jnp.float32
    qkv, z, b, a = _split(h @ w_in, [GDN_CONV_DIM, GDN_VAL_DIM, GDN_V_HEADS, GDN_V_HEADS])
    qkv = jax.nn.silu(causal_dwconv(qkv, conv_w))
    q, k, v = _split(qkv, [GDN_KEY_DIM, GDN_KEY_DIM, GDN_VAL_DIM])
    rep = GDN_V_HEADS // GDN_QK_HEADS
    q = jnp.repeat(l2_normalize(q.reshape(bsz, seq, GDN_QK_HEADS, GDN_HEAD_DIM)) * GDN_HEAD_DIM ** -0.5, rep, axis=2)
    k = jnp.repeat(l2_normalize(k.reshape(bsz, seq, GDN_QK_HEADS, GDN_HEAD_DIM)), rep, axis=2)
    v = v.reshape(bsz, seq, GDN_V_HEADS, GDN_HEAD_DIM).astype(f32)
    beta = jax.nn.sigmoid(b.astype(f32))
    g = -jnp.exp(a_log.astype(f32)) * jax.nn.softplus(a.astype(f32) + dt_bias.astype(f32))
    o = gated_delta_chunked(q, k, v, beta, g)
    o = rms_norm(o, norm_w) * jax.nn.silu(z.astype(f32).reshape(bsz, seq, GDN_V_HEADS, GDN_HEAD_DIM))
    return o.reshape(bsz, seq, GDN_VAL_DIM).astype(h.dtype) @ w_out


def sgu_mixer(h, w_in, norm_w, w_s, b_s, w_out):
    bsz, seq, _ = h.shape
    u, v = _split(jax.nn.gelu(h @ w_in), [SGU_HALF, SGU_HALF])
    v = rms_norm(v, norm_w).reshape(bsz, seq // SGU_CHUNK, SGU_CHUNK, SGU_GROUPS, SGU_GROUP_DIM)
    w = jnp.tril(w_s).astype(v.dtype)
    v = jnp.einsum('gij,bcjgd->bcigd', w, v) + jnp.swapaxes(b_s, 0, 1)[None, None, :, :, None].astype(v.dtype)
    return (u * v.reshape(bsz, seq, SGU_HALF)) @ w_out


def conv_ffn(h, w_gu, conv_w, conv_b, w_down):
    gate, up = _split(h @ w_gu, [FFN_HIDDEN, FFN_HIDDEN])
    return (jax.nn.silu(causal_dwconv(gate, conv_w, conv_b)) * up) @ w_down


def setup_inputs(seed: int = 0) -> dict:
    key = jax.random.key(seed)
    keys = jax.random.split(key, 40)
    counter = [0]
    f32 = jnp.float32

    def nk():
        counter[0] += 1
        return keys[counter[0] - 1]

    def dense(shape, fan_in):
        return jax.random.normal(nk(), shape, f32) * (fan_in ** -0.5)

    def gain(shape):
        return 1.0 + 0.02 * jax.random.normal(nk(), shape, f32)

    def small(shape):
        return 0.02 * jax.random.normal(nk(), shape, f32)

    def a_log(shape):
        return jnp.log(jax.random.uniform(nk(), shape, f32, 1.0, 16.0))

    def dt_bias(shape):
        dt = jnp.exp(jax.random.uniform(nk(), shape, f32, float(np.log(1e-3)), float(np.log(1e-1))))
        return dt + jnp.log(-jnp.expm1(-dt))

    na, nb, nc, nd = [len(range(m, DEPTH, N_MIXERS)) for m in range(N_MIXERS)]
    D = D_MODEL
    return {
        'x': jax.random.normal(nk(), (BATCH, SEQ, D), f32),
        'mix_norm': gain((DEPTH, D)),
        'ffn_norm': gain((DEPTH, D)),
        'ssm_w_in': dense((na, D, SSM_IN_DIM), D),
        'ssm_conv_w': dense((na, SSM_CONV, SSM_CONV_DIM), SSM_CONV),
        'ssm_conv_b': small((na, SSM_CONV_DIM)),
        'ssm_dt_bias': dt_bias((na, SSM_N_HEADS)),
        'ssm_a_log': a_log((na, SSM_N_HEADS)),
        'ssm_d': gain((na, SSM_N_HEADS)),
        'ssm_norm': gain((na, SSM_D_INNER)),
        'ssm_w_out': dense((na, SSM_D_INNER, D), SSM_D_INNER),
        'nsa_w_in': dense((nb, D, NSA_IN_DIM), D),
        'nsa_q_norm': gain((nb, NSA_HEAD_DIM)),
        'nsa_k_norm': gain((nb, NSA_HEAD_DIM)),
        'nsa_cmp_pos': dense((nb, NSA_CMP_BLOCK, NSA_HEAD_DIM), 16),
        'nsa_cmp_k_w1': dense((nb, NSA_CMP_BLOCK * NSA_HEAD_DIM, NSA_CMP_HIDDEN), NSA_CMP_BLOCK * NSA_HEAD_DIM),
        'nsa_cmp_k_w2': dense((nb, NSA_CMP_HIDDEN, NSA_HEAD_DIM), NSA_CMP_HIDDEN),
        'nsa_cmp_v_w1': dense((nb, NSA_CMP_BLOCK * NSA_HEAD_DIM, NSA_CMP_HIDDEN), NSA_CMP_BLOCK * NSA_HEAD_DIM),
        'nsa_cmp_v_w2': dense((nb, NSA_CMP_HIDDEN, NSA_HEAD_DIM), NSA_CMP_HIDDEN),
        'nsa_w_out': dense((nb, NSA_N_HEADS * NSA_HEAD_DIM, D), NSA_N_HEADS * NSA_HEAD_DIM),
        'gdn_w_in': dense((nc, D, GDN_IN_DIM), D),
        'gdn_conv_w': dense((nc, GDN_CONV, GDN_CONV_DIM), GDN_CONV),
        'gdn_a_log': a_log((nc, GDN_V_HEADS)),
        'gdn_dt_bias': dt_bias((nc, GDN_V_HEADS)),
        'gdn_norm': gain((nc, GDN_HEAD_DIM)),
        'gdn_w_out': dense((nc, GDN_VAL_DIM, D), GDN_VAL_DIM),
        'sgu_w_in': dense((nd, D, 2 * SGU_HALF), D),
        'sgu_norm': gain((nd, SGU_HALF)),
        'sgu_w_s': dense((nd, SGU_GROUPS, SGU_CHUNK, SGU_CHUNK), SGU_CHUNK),
        'sgu_b_s': gain((nd, SGU_GROUPS, SGU_CHUNK)),
        'sgu_w_out': dense((nd, SGU_HALF, D), SGU_HALF),
        'ffn_w_gu': dense((DEPTH, D, 2 * FFN_HIDDEN), D),
        'ffn_conv_w': dense((DEPTH, FFN_CONV, FFN_HIDDEN), FFN_CONV),
        'ffn_conv_b': small((DEPTH, FFN_HIDDEN)),
        'ffn_w_down': dense((DEPTH, FFN_HIDDEN, D), FFN_HIDDEN),
    }


def reference(x, mix_norm, ffn_norm,
              ssm_w_in, ssm_conv_w, ssm_conv_b, ssm_dt_bias, ssm_a_log, ssm_d, ssm_norm, ssm_w_out,
              nsa_w_in, nsa_q_norm, nsa_k_norm, nsa_cmp_pos, nsa_cmp_k_w1, nsa_cmp_k_w2, nsa_cmp_v_w1, nsa_cmp_v_w2, nsa_w_out,
              gdn_w_in, gdn_conv_w, gdn_a_log, gdn_dt_bias, gdn_norm, gdn_w_out,
              sgu_w_in, sgu_norm, sgu_w_s, sgu_b_s, sgu_w_out,
              ffn_w_gu, ffn_conv_w, ffn_conv_b, ffn_w_down):
    for i in range(DEPTH):
        m, j = i % N_MIXERS, i // N_MIXERS
        h = rms_norm(x, mix_norm[i])
        if m == 0:
            y = mamba2_mixer(h, ssm_w_in[j], ssm_conv_w[j], ssm_conv_b[j], ssm_dt_bias[j], ssm_a_log[j],
                             ssm_d[j], ssm_norm[j], ssm_w_out[j])
        elif m == 1:
            y = nsa_mixer(h, nsa_w_in[j], nsa_q_norm[j], nsa_k_norm[j], nsa_cmp_pos[j], nsa_cmp_k_w1[j],
                          nsa_cmp_k_w2[j], nsa_cmp_v_w1[j], nsa_cmp_v_w2[j], nsa_w_out[j])
        elif m == 2:
            y = gdn_mixer(h, gdn_w_in[j], gdn_conv_w[j], gdn_a_log[j], gdn_dt_bias[j], gdn_norm[j], gdn_w_out[j])
        else:
            y = sgu_mixer(h, sgu_w_in[j], sgu_norm[j], sgu_w_s[j], sgu_b_s[j], sgu_w_out[j])
        x = x + y.astype(x.dtype)
        x = x + conv_ffn(rms_norm(x, ffn_norm[i]), ffn_w_gu[i], ffn_conv_w[i], ffn_conv_b[i], ffn_w_down[i]).astype(x.dtype)
    return x
```

```python
import functools

import jax
import jax.numpy as jnp
import numpy as np
from jax import lax
from jax.experimental import pallas as pl
from jax.experimental.pallas import tpu as pltpu

F32 = jnp.float32
BF16 = jnp.bfloat16
EPS = 1e-6
HI = lax.Precision.HIGHEST

V7X_VMEM_BYTES = 64 * 1024 * 1024
VMEM_LIMIT = 48 * 1024 * 1024
BF16_SUBLANES = 16

FFN_HIDDEN = 5632
SSM_HEAD_DIM = 64
SSM_GROUPS = 8
SSM_STATE = 128
SSM_CHUNK = 128
SGU_CHUNK = 128
SGU_GROUPS = 16
GDN_HEAD_DIM = 128
GDN_QK_HEADS = 16
GDN_CHUNK = 64
NSA_GROUPS = 4
NSA_HPG = 4
NSA_DH = 128
NSA_SEL_BLOCK = 64
NSA_TOP_N = 16
NSA_WINDOW = 512
NSA_FORCE = 1e4


def _cparams(sem):
    return pltpu.CompilerParams(dimension_semantics=sem, vmem_limit_bytes=VMEM_LIMIT)


def _dot(a, b):
    return jnp.dot(a, b, preferred_element_type=F32)


def _dot_nt(a, b):
    return lax.dot_general(a, b, (((1,), (1,)), ((), ())), preferred_element_type=F32)


def _dot_tn(a, b):
    return lax.dot_general(a, b, (((0,), (0,)), ((), ())), preferred_element_type=F32)


def _silu(x):
    return x * jax.nn.sigmoid(x)


def _rms_matmul_kernel(*refs, act, has_small):
    if has_small:
        x_ref, g_ref, w_ref, ws_ref, o_ref, os_ref, xn_ref = refs
    else:
        x_ref, g_ref, w_ref, o_ref, xn_ref = refs

    @pl.when(pl.program_id(1) == 0)
    def _():
        x = x_ref[...]
        y = x * lax.rsqrt(jnp.mean(x * x, axis=-1, keepdims=True) + EPS)
        xn = (y * g_ref[...]).astype(BF16)
        xn_ref[...] = xn
        if has_small:
            os_ref[...] = _dot(xn, ws_ref[...])

    y = _dot(xn_ref[...], w_ref[...])
    if act == "gelu":
        y = jax.nn.gelu(y)
    o_ref[...] = y.astype(o_ref.dtype)


def rms_matmul(x, gain, w, w_small=None, *, act=None, tm=512, tn=512):
    t, d = x.shape
    n = w.shape[1]
    has_small = w_small is not None
    in_specs = [
        pl.BlockSpec((tm, d), lambda i, j: (i, 0)),
        pl.BlockSpec((1, d), lambda i, j: (0, 0)),
        pl.BlockSpec((d, tn), lambda i, j: (0, j)),
    ]
    out_shape = [jax.ShapeDtypeStruct((t, n), BF16)]
    out_specs = [pl.BlockSpec((tm, tn), lambda i, j: (i, j))]
    args = [x, gain.reshape(1, d), w]
    if has_small:
        ns = w_small.shape[1]
        in_specs.append(pl.BlockSpec((d, ns), lambda i, j: (0, 0)))
        out_shape.append(jax.ShapeDtypeStruct((t, ns), F32))
        out_specs.append(pl.BlockSpec((tm, ns), lambda i, j: (i, 0)))
        args.append(w_small)
    res = pl.pallas_call(
        functools.partial(_rms_matmul_kernel, act=act, has_small=has_small),
        grid=(t // tm, n // tn),
        in_specs=in_specs,
        out_specs=out_specs,
        out_shape=out_shape,
        scratch_shapes=[pltpu.VMEM((tm, d), BF16)],
        compiler_params=_cparams(("parallel", "arbitrary")),
        name="rms_matmul",
    )(*args)
    return res if has_small else res[0]


def _matmul_resid_kernel(a_ref, w_ref, r_ref, o_ref):
    @pl.when(pl.program_id(1) == 0)
    def _():
        o_ref[...] = r_ref[...]

    o_ref[...] += _dot(a_ref[...], w_ref[...])


def matmul_resid(a, w, resid, *, tm=512, tk=512):
    t, k = a.shape
    n = w.shape[1]
    return pl.pallas_call(
        _matmul_resid_kernel,
        grid=(t // tm, k // tk),
        in_specs=[
            pl.BlockSpec((tm, tk), lambda i, j: (i, j)),
            pl.BlockSpec((tk, n), lambda i, j: (j, 0)),
            pl.BlockSpec((tm, n), lambda i, j: (i, 0)),
        ],
        out_specs=pl.BlockSpec((tm, n), lambda i, j: (i, 0)),
        out_shape=jax.ShapeDtypeStruct((t, n), F32),
        compiler_params=_cparams(("parallel", "arbitrary")),
        name="matmul_resid",
    )(a, w, resid)


def _shift_rows(cur, prev_tail, s):
    rows = cur.shape[0]
    pr = prev_tail.shape[0]
    rolled = pltpu.roll(cur, s, 0)
    ridx = lax.broadcasted_iota(jnp.int32, cur.shape, 0)
    for r in range(s):
        rolled = jnp.where(ridx == r, prev_tail[pr - s + r:pr - s + r + 1, :], rolled)
    return rolled


def _causal_conv(cur, prev_tail, w_ref, width):
    acc = cur * w_ref[width - 1:width, :]
    for s in range(1, width):
        acc = acc + _shift_rows(cur, prev_tail, s) * w_ref[width - 1 - s:width - s, :]
    return acc


def _ffn_down_kernel(g_ref, gh_ref, u_ref, cw_ref, cb_ref, w_ref, r_ref, o_ref, *, tm, seq):
    i = pl.program_id(0)

    @pl.when(pl.program_id(1) == 0)
    def _():
        o_ref[...] = r_ref[...]

    g = g_ref[...].astype(F32)
    first = (i * tm) % seq == 0
    halo = jnp.where(first, 0.0, gh_ref[...].astype(F32))
    c = _causal_conv(g, halo, cw_ref, 3) + cb_ref[...]
    a = (_silu(c) * u_ref[...].astype(F32)).astype(BF16)
    o_ref[...] += _dot(a, w_ref[...])


def ffn_down(gu, conv_w, conv_b, w_down, resid, *, seq, tm=512, tk=512):
    t = gu.shape[0]
    hid = w_down.shape[0]
    n = w_down.shape[1]
    nk = hid // tk
    hb = tm // BF16_SUBLANES
    return pl.pallas_call(
        functools.partial(_ffn_down_kernel, tm=tm, seq=seq),
        grid=(t // tm, nk),
        in_specs=[
            pl.BlockSpec((tm, tk), lambda i, j: (i, j)),
            pl.BlockSpec((BF16_SUBLANES, tk), lambda i, j: (jnp.maximum(i * hb - 1, 0), j)),
            pl.BlockSpec((tm, tk), lambda i, j: (i, j + nk)),
            pl.BlockSpec((3, tk), lambda i, j: (0, j)),
            pl.BlockSpec((1, tk), lambda i, j: (0, j)),
            pl.BlockSpec((tk, n), lambda i, j: (j, 0)),
            pl.BlockSpec((tm, n), lambda i, j: (i, 0)),
        ],
        out_specs=pl.BlockSpec((tm, n), lambda i, j: (i, 0)),
        out_shape=jax.ShapeDtypeStruct((t, n), F32),
        compiler_params=_cparams(("parallel", "arbitrary")),
        name="ffn_down",
    )(gu, gu, gu, conv_w, conv_b.reshape(1, hid), w_down, resid)


def conv_ffn_layer(x, norm_w, w_gu, conv_w, conv_b, w_down, *, seq):
    gu = rms_matmul(x, norm_w, w_gu.astype(BF16))
    return ffn_down(gu, conv_w, conv_b, w_down.astype(BF16), x, seq=seq)


def _sgu_kernel(u_ref, v_ref, nw_ref, ws_ref, bs_ref, o_ref):
    c = SGU_CHUNK
    v = v_ref[...].astype(F32)
    vn = v * lax.rsqrt(jnp.mean(v * v, axis=-1, keepdims=True) + EPS) * nw_ref[...]
    vn = vn.astype(BF16)
    gd = v.shape[1] // SGU_GROUPS
    ri = lax.broadcasted_iota(jnp.int32, (c, c), 0)
    ci = lax.broadcasted_iota(jnp.int32, (c, c), 1)
    for g in range(SGU_GROUPS):
        w = jnp.where(ci <= ri, ws_ref[g], 0.0).astype(BF16)
        sl = slice(g * gd, (g + 1) * gd)
        mixed = _dot(w, vn[:, sl]) + bs_ref[:, g:g + 1]
        o_ref[:, sl] = (u_ref[:, sl].astype(F32) * mixed).astype(BF16)


def sgu_mix(uv, norm_w, w_s, b_s):
    t = uv.shape[0]
    half = uv.shape[1] // 2
    c = SGU_CHUNK
    return pl.pallas_call(
        _sgu_kernel,
        grid=(t // c,),
        in_specs=[
            pl.BlockSpec((c, half), lambda i: (i, 0)),
            pl.BlockSpec((c, half), lambda i: (i, 1)),
            pl.BlockSpec((1, half), lambda i: (0, 0)),
            pl.BlockSpec((SGU_GROUPS, c, c), lambda i: (0, 0, 0)),
            pl.BlockSpec((c, SGU_GROUPS), lambda i: (0, 0)),
        ],
        out_specs=pl.BlockSpec((c, half), lambda i: (i, 0)),
        out_shape=jax.ShapeDtypeStruct((t, half), BF16),
        compiler_params=_cparams(("parallel",)),
        name="sgu_mix",
    )(uv, uv, norm_w.reshape(1, half), w_s, b_s.T)


def sgu_layer(x, norm_w, w_in, sgu_norm, w_s, b_s, w_out):
    uv = rms_matmul(x, norm_w, w_in.astype(BF16), act="gelu")
    a = sgu_mix(uv, sgu_norm, w_s, b_s)
    return matmul_resid(a, w_out.astype(BF16), x)


def _expand_pairs(f, width):
    r = f.shape[0]
    per = 128 // width
    lane = lax.broadcasted_iota(jnp.int32, (r, 128), 1)
    parts = []
    for k in range(8 // per):
        blk = jnp.broadcast_to(f[:, per * k:per * k + 1], (r, 128))
        for q in range(1, per):
            blk = jnp.where(lane >= q * width, f[:, per * k + q:per * k + q + 1], blk)
        parts.append(blk)
    return jnp.concatenate(parts, axis=1)


def _ssd_kernel(z_ref, x_ref, b_ref, c_ref, dt_ref, cw_ref, cb_ref, dtb_ref, alog_ref, dsk_ref, nw_ref,
                o_ref, state_ref, tail_ref):
    g = pl.program_id(1)
    L = SSM_CHUNK
    P = SSM_HEAD_DIM
    hw = 8 * P

    @pl.when(pl.program_id(2) == 0)
    def _():
        state_ref[...] = jnp.zeros_like(state_ref)
        tail_ref[...] = jnp.zeros_like(tail_ref)

    cat = jnp.concatenate([x_ref[...], b_ref[...], c_ref[...]], axis=1).astype(F32)
    act = _silu(_causal_conv(cat, tail_ref[...], cw_ref, 4) + cb_ref[...])
    tail_ref[...] = cat[L - 8:L, :]
    xs = act[:, :hw]
    bm = act[:, hw:hw + SSM_STATE].astype(BF16)
    cm = act[:, hw + SSM_STATE:].astype(BF16)

    dt = pltpu.roll(dt_ref[...], (128 - 8 * g) % 128, 1)
    dt = jax.nn.softplus(dt + dtb_ref[...])
    a = -jnp.exp(alog_ref[...])
    ri = lax.broadcasted_iota(jnp.int32, (L, L), 0)
    ci = lax.broadcasted_iota(jnp.int32, (L, L), 1)
    causal = ci <= ri
    cum = jnp.dot(causal.astype(F32), dt * a, precision=HI, preferred_element_type=F32)
    cum_t = cum.T
    last = cum[L - 1:L, :]
    ecum = jnp.exp(cum)
    wts = jnp.exp(last - cum) * dt
    elast = jnp.exp(last)

    xdt = xs * _expand_pairs(dt, P)
    cb = _dot_nt(cm, bm)
    lane = lax.broadcasted_iota(jnp.int32, (L, 128), 1)
    ys = []
    for k in range(4):
        xpair = xdt[:, 128 * k:128 * (k + 1)]
        acc = None
        for q in range(2):
            h = 2 * k + q
            diff = cum[:, h:h + 1] - cum_t[h:h + 1, :]
            w = (cb * jnp.exp(jnp.where(causal, diff, -1e30))).astype(BF16)
            xm = jnp.where((lane >= P) == (q == 1), xpair, 0.0).astype(BF16)
            part = _dot(w, xm)
            acc = part if acc is None else acc + part
        ys.append(acc)
    y = jnp.concatenate(ys, axis=1)
    st = state_ref[...]
    y = y + _dot(cm, st.astype(BF16)) * _expand_pairs(ecum, P)
    state_ref[...] = st * _expand_pairs(elast, P) + _dot_tn(bm, (xs * _expand_pairs(wts, P)).astype(BF16))

    y = y + dsk_ref[...] * xs
    y = y * _silu(z_ref[...].astype(F32))
    y = y * lax.rsqrt(jnp.mean(y * y, axis=-1, keepdims=True) + EPS) * nw_ref[...]
    o_ref[...] = y.astype(BF16)


def ssd_scan(zx, dt_raw, conv_w, conv_b, dt_bias, a_log, d_skip, norm_w, *, bsz, seq):
    t = zx.shape[0]
    G, P, N, L = SSM_GROUPS, SSM_HEAD_DIM, SSM_STATE, SSM_CHUNK
    hw = 8 * P
    d_inner = G * hw
    nc = seq // L
    cw = jnp.concatenate([conv_w[:, :d_inner].reshape(4, G, hw),
                          conv_w[:, d_inner:d_inner + G * N].reshape(4, G, N),
                          conv_w[:, d_inner + G * N:].reshape(4, G, N)], axis=2).transpose(1, 0, 2)
    cb = jnp.concatenate([conv_b[:d_inner].reshape(G, 1, hw),
                          conv_b[d_inner:d_inner + G * N].reshape(G, 1, N),
                          conv_b[d_inner + G * N:].reshape(G, 1, N)], axis=2)
    pad8 = lambda v: jnp.pad(v.reshape(G, 1, 8), ((0, 0), (0, 0), (0, 120)))
    dsk = jnp.repeat(d_skip, P).reshape(G, 1, hw)
    cwid = hw + 2 * N
    row = lambda b, g, c: b * nc + c
    return pl.pallas_call(
        _ssd_kernel,
        grid=(bsz, G, nc),
        in_specs=[
            pl.BlockSpec((L, hw), lambda b, g, c: (row(b, g, c), g)),
            pl.BlockSpec((L, hw), lambda b, g, c: (row(b, g, c), G + g)),
            pl.BlockSpec((L, N), lambda b, g, c: (row(b, g, c), 2 * d_inner // N + g)),
            pl.BlockSpec((L, N), lambda b, g, c: (row(b, g, c), 2 * d_inner // N + G + g)),
            pl.BlockSpec((L, 128), lambda b, g, c: (row(b, g, c), 0)),
            pl.BlockSpec((None, 4, cwid), lambda b, g, c: (g, 0, 0)),
            pl.BlockSpec((None, 1, cwid), lambda b, g, c: (g, 0, 0)),
            pl.BlockSpec((None, 1, 128), lambda b, g, c: (g, 0, 0)),
            pl.BlockSpec((None, 1, 128), lambda b, g, c: (g, 0, 0)),
            pl.BlockSpec((None, 1, hw), lambda b, g, c: (g, 0, 0)),
            pl.BlockSpec((None, 1, hw), lambda b, g, c: (g, 0, 0)),
        ],
        out_specs=pl.BlockSpec((L, hw), lambda b, g, c: (row(b, g, c), g)),
        out_shape=jax.ShapeDtypeStruct((t, d_inner), BF16),
        scratch_shapes=[pltpu.VMEM((N, hw), F32), pltpu.VMEM((8, cwid), F32)],
        compiler_params=_cparams(("parallel", "parallel", "arbitrary")),
        name="ssd_scan",
    )(zx, zx, zx, zx, dt_raw, cw, cb, pad8(dt_bias), pad8(a_log), dsk, norm_w.reshape(G, 1, hw))


def ssd_layer(x, norm_w, w_in, conv_w, conv_b, dt_bias, a_log, d_skip, ssm_norm, w_out, *, bsz, seq):
    nmain = w_in.shape[1] - SSM_GROUPS * 8
    w_dt = jnp.pad(w_in[:, nmain:], ((0, 0), (0, 128 - SSM_GROUPS * 8)))
    zx, dt_raw = rms_matmul(x, norm_w, w_in[:, :nmain].astype(BF16), w_dt.astype(BF16))
    y = ssd_scan(zx, dt_raw, conv_w, conv_b, dt_bias, a_log, d_skip, ssm_norm, bsz=bsz, seq=seq)
    return matmul_resid(y, w_out.astype(BF16), x)


GDN_STEP = 128


def _unit_lower_inverse(nmat, eye):
    c = nmat.shape[0]
    p = -nmat
    x = eye + p
    covered = 2
    while covered < c:
        p = jnp.dot(p, p, precision=HI, preferred_element_type=F32)
        x = x + jnp.dot(x, p, precision=HI, preferred_element_type=F32)
        covered *= 2
    return x


def _gdn_kernel(q_ref, k_ref, v_ref, z_ref, ba_ref, cw_ref, alog_ref, dtb_ref, nw_ref,
                o_ref, state_ref, tail_ref):
    p = pl.program_id(1)
    Ls, C, D = GDN_STEP, GDN_CHUNK, GDN_HEAD_DIM

    @pl.when(pl.program_id(2) == 0)
    def _():
        state_ref[...] = jnp.zeros_like(state_ref)
        tail_ref[...] = jnp.zeros_like(tail_ref)

    cat = jnp.concatenate([q_ref[...], k_ref[...], v_ref[...]], axis=1).astype(F32)
    act = _silu(_causal_conv(cat, tail_ref[...], cw_ref, 4))
    tail_ref[...] = cat[Ls - 8:Ls, :]
    q = act[:, :D]
    k = act[:, D:2 * D]
    v = act[:, 2 * D:]
    qn = q * lax.rsqrt(jnp.sum(q * q, axis=-1, keepdims=True) + EPS) * (D ** -0.5)
    kn = k * lax.rsqrt(jnp.sum(k * k, axis=-1, keepdims=True) + EPS)
    qb = qn.astype(BF16)
    kb = kn.astype(BF16)

    braw = pltpu.roll(ba_ref[...], (128 - 2 * p) % 128, 1)
    araw = pltpu.roll(braw, 128 - 2 * GDN_QK_HEADS, 1)
    beta = jax.nn.sigmoid(braw)
    gl = -jnp.exp(alog_ref[...]) * jax.nn.softplus(araw + dtb_ref[...])
    ri = lax.broadcasted_iota(jnp.int32, (Ls, Ls), 0)
    ci = lax.broadcasted_iota(jnp.int32, (Ls, Ls), 1)
    seg = ((ri // C) == (ci // C)) & (ci <= ri)
    cum = jnp.dot(seg.astype(F32), gl, precision=HI, preferred_element_type=F32)
    cum_t = cum.T
    ecum = jnp.exp(cum)

    r2 = lax.broadcasted_iota(jnp.int32, (C, C), 0)
    c2 = lax.broadcasted_iota(jnp.int32, (C, C), 1)
    incl = c2 <= r2
    strict = c2 < r2
    eye = (r2 == c2).astype(F32)

    for ch in range(Ls // C):
        rows = slice(ch * C, (ch + 1) * C)
        kc = kb[rows]
        qc = qb[rows]
        kk = _dot_nt(kc, kc)
        qk = _dot_nt(qc, kc)
        for h in range(2):
            cc = cum[rows, h:h + 1]
            cr = cum_t[h:h + 1, rows]
            dec = jnp.exp(jnp.where(incl, cc - cr, -1e30))
            bcol = beta[rows, h:h + 1]
            ec = ecum[rows, h:h + 1]
            tinv = _unit_lower_inverse(jnp.where(strict, bcol * kk * dec, 0.0), eye)
            vh = v[rows, h * D:(h + 1) * D]
            rhs = jnp.concatenate([vh * bcol, kn[rows] * (bcol * ec)], axis=1)
            sol = jnp.dot(tinv, rhs, precision=HI, preferred_element_type=F32)
            u = sol[:, :D]
            w = sol[:, D:]
            s = state_ref[h]
            sb = s.astype(BF16)
            v_new = u - _dot(w.astype(BF16), sb)
            vnb = v_new.astype(BF16)
            attn = jnp.where(incl, qk * dec, 0.0).astype(BF16)
            out = _dot((qn[rows] * ec).astype(BF16), sb) + _dot(attn, vnb)
            last = cum[(ch + 1) * C - 1:(ch + 1) * C, h:h + 1]
            state_ref[h] = s * jnp.exp(last) + _dot_tn((kn[rows] * jnp.exp(last - cc)).astype(BF16), vnb)
            on = out * lax.rsqrt(jnp.mean(out * out, axis=-1, keepdims=True) + EPS) * nw_ref[...]
            zc = z_ref[rows, h * D:(h + 1) * D].astype(F32)
            o_ref[rows, h * D:(h + 1) * D] = (on * _silu(zc)).astype(BF16)


def gdn_scan(qkvz, ba_raw, conv_w, a_log, dt_bias, norm_w, *, bsz, seq):
    t = qkvz.shape[0]
    H, D, Ls = GDN_QK_HEADS, GDN_HEAD_DIM, GDN_STEP
    kd = H * D
    ns = seq // Ls
    cw = jnp.concatenate([conv_w[:, :kd].reshape(4, H, D), conv_w[:, kd:2 * kd].reshape(4, H, D),
                          conv_w[:, 2 * kd:].reshape(4, H, 2 * D)], axis=2).transpose(1, 0, 2)
    pad2 = lambda v: jnp.pad(v.reshape(H, 1, 2), ((0, 0), (0, 0), (0, 126)))
    row = lambda b, p, c: b * ns + c
    return pl.pallas_call(
        _gdn_kernel,
        grid=(bsz, H, ns),
        in_specs=[
            pl.BlockSpec((Ls, D), lambda b, p, c: (row(b, p, c), p)),
            pl.BlockSpec((Ls, D), lambda b, p, c: (row(b, p, c), H + p)),
            pl.BlockSpec((Ls, 2 * D), lambda b, p, c: (row(b, p, c), H + p)),
            pl.BlockSpec((Ls, 2 * D), lambda b, p, c: (row(b, p, c), 2 * H + p)),
            pl.BlockSpec((Ls, 128), lambda b, p, c: (row(b, p, c), 0)),
            pl.BlockSpec((None, 4, 4 * D), lambda b, p, c: (p, 0, 0)),
            pl.BlockSpec((None, 1, 128), lambda b, p, c: (p, 0, 0)),
            pl.BlockSpec((None, 1, 128), lambda b, p, c: (p, 0, 0)),
            pl.BlockSpec((1, D), lambda b, p, c: (0, 0)),
        ],
        out_specs=pl.BlockSpec((Ls, 2 * D), lambda b, p, c: (row(b, p, c), p)),
        out_shape=jax.ShapeDtypeStruct((t, 2 * kd), BF16),
        scratch_shapes=[pltpu.VMEM((2, D, D), F32), pltpu.VMEM((8, 4 * D), F32)],
        compiler_params=_cparams(("parallel", "parallel", "arbitrary")),
        name="gdn_scan",
    )(qkvz, qkvz, qkvz, qkvz, ba_raw, cw, pad2(a_log), pad2(dt_bias), norm_w.reshape(1, D))


def gdn_layer(x, norm_w, w_in, conv_w, a_log, dt_bias, gdn_norm, w_out, *, bsz, seq):
    nmain = w_in.shape[1] - 4 * GDN_QK_HEADS
    w_ba = jnp.pad(w_in[:, nmain:], ((0, 0), (0, 128 - 4 * GDN_QK_HEADS)))
    qkvz, ba_raw = rms_matmul(x, norm_w, w_in[:, :nmain].astype(BF16), w_ba.astype(BF16))
    y = gdn_scan(qkvz, ba_raw, conv_w, a_log, dt_bias, gdn_norm, bsz=bsz, seq=seq)
    return matmul_resid(y, w_out.astype(BF16), x)


NSA_STRIDE = 16
NSA_QT = 128
NSA_KT = 512
NEG = -1e30


def _nsa_compress_kernel(kc_ref, vc_ref, pos_ref, kw1_ref, kw2_ref, vw1_ref, vw2_ref, kn_ref, ko_ref, vo_ref):
    half = pos_ref.shape[1] // 2
    nb = kc_ref.shape[0]

    def compress(x_ref, w1_ref, w2_ref):
        x = x_ref[...].astype(F32)
        lo = _dot((x + pos_ref[:, :half]).astype(BF16), w1_ref[:half, :])
        hi = _dot((x + pos_ref[:, half:]).astype(BF16), w1_ref[half:, :])
        hid = _silu(lo + pltpu.roll(hi, nb - 1, 0))
        return _dot(hid.astype(BF16), w2_ref[...])

    kc = compress(kc_ref, kw1_ref, kw2_ref)
    kc = kc * lax.rsqrt(jnp.mean(kc * kc, axis=-1, keepdims=True) + EPS) * kn_ref[...]
    ko_ref[...] = kc.astype(BF16)
    vo_ref[...] = compress(vc_ref, vw1_ref, vw2_ref).astype(BF16)


def nsa_compress(kc_t, vc_t, pos, k_w1, k_w2, v_w1, v_w2, k_norm):
    bsz, G, nb, wid = kc_t.shape
    dh = NSA_DH
    hid = k_w1.shape[1]
    full = lambda *shape: pl.BlockSpec(shape, lambda b, g: (0,) * len(shape))
    blk = pl.BlockSpec((None, None, nb, wid), lambda b, g: (b, g, 0, 0))
    oblk = pl.BlockSpec((None, None, nb, dh), lambda b, g: (b, g, 0, 0))
    return pl.pallas_call(
        _nsa_compress_kernel,
        grid=(bsz, G),
        in_specs=[blk, blk, full(1, 2 * wid), full(2 * wid, hid), full(hid, dh), full(2 * wid, hid), full(hid, dh),
                  full(1, dh)],
        out_specs=[oblk, oblk],
        out_shape=[jax.ShapeDtypeStruct((bsz, G, nb, dh), BF16)] * 2,
        compiler_params=_cparams(("parallel", "parallel")),
        name="nsa_compress",
    )(kc_t, vc_t, pos.reshape(1, 2 * wid), k_w1.astype(BF16), k_w2.astype(BF16), v_w1.astype(BF16),
      v_w2.astype(BF16), k_norm.reshape(1, dh))


def _softmax_rows(s, mask):
    s = jnp.where(mask, s, NEG)
    p = jnp.where(mask, jnp.exp(s - jnp.max(s, axis=-1, keepdims=True)), 0.0)
    den = jnp.sum(p, axis=-1, keepdims=True)
    return p / jnp.where(den > 0, den, 1.0)


def _nsa_attn_kernel(q_ref, ks_ref, vs_ref, kw_ref, vw_ref, kc_ref, vc_ref, gt_ref, qn_ref, kn_ref, o_ref):
    g = pl.program_id(1)
    i = pl.program_id(2)
    TQ, HG, DH, KT = NSA_QT, NSA_HPG, NSA_DH, NSA_KT
    R = HG * TQ
    q0 = i * TQ
    scale = DH ** -0.5

    def norm_k(kraw):
        kf = kraw.astype(F32)
        return (kf * lax.rsqrt(jnp.mean(kf * kf, axis=-1, keepdims=True) + EPS) * kn_ref[...]).astype(BF16)

    qs = []
    for h in range(HG):
        qh = q_ref[:, h * DH:(h + 1) * DH].astype(F32)
        qh = qh * lax.rsqrt(jnp.mean(qh * qh, axis=-1, keepdims=True) + EPS) * qn_ref[...]
        qs.append((qh * scale).astype(BF16))
    qall = jnp.concatenate(qs, axis=0)

    def tok(shape):
        return q0 + lax.broadcasted_iota(jnp.int32, shape, 0) % TQ

    ncmp = kc_ref.shape[0]
    s_c = _dot_nt(qall, kc_ref[...])
    n_idx = lax.broadcasted_iota(jnp.int32, (R, ncmp), 1)
    p_c = _softmax_rows(s_c, NSA_STRIDE * n_idx + 2 * NSA_STRIDE - 1 <= tok((R, ncmp)))
    o_c = _dot(p_c.astype(BF16), vc_ref[...])

    NS = 128
    psum = p_c[0:TQ]
    for h in range(1, HG):
        psum = psum + p_c[h * TQ:(h + 1) * TQ]
    cn = lax.broadcasted_iota(jnp.int32, (ncmp, NS), 0)
    cm = lax.broadcasted_iota(jnp.int32, (ncmp, NS), 1)
    per = NSA_SEL_BLOCK // NSA_STRIDE
    cover = ((cn >= per * cm - 1) & (cn <= per * cm + per - 1)).astype(F32)
    imp = jnp.dot(psum, cover, precision=HI, preferred_element_type=F32)
    t_s = tok((TQ, NS))
    m_s = lax.broadcasted_iota(jnp.int32, (TQ, NS), 1)
    blk = t_s // NSA_SEL_BLOCK
    forced = (m_s == 0) | (m_s == blk) | (m_s == blk - 1)
    valid = m_s * NSA_SEL_BLOCK <= t_s
    score = jnp.where(valid, jnp.where(forced, NSA_FORCE, imp), -NSA_FORCE)
    lane_f = m_s.astype(F32)
    sel = jnp.zeros((TQ, NS), F32)
    for _ in range(NSA_TOP_N):
        mx = jnp.max(score, axis=-1, keepdims=True)
        first = jnp.min(jnp.where(score == mx, lane_f, float(NS)), axis=-1, keepdims=True)
        pick = lane_f == first
        sel = jnp.where(pick, 1.0, sel)
        score = jnp.where(pick, -3e38, score)
    sel_b = sel.astype(BF16)

    e_m = lax.broadcasted_iota(jnp.int32, (NS, KT), 0)
    e_j = lax.broadcasted_iota(jnp.int32, (NS, KT), 1) // NSA_SEL_BLOCK
    t_k = tok((R, KT))
    j_k = lax.broadcasted_iota(jnp.int32, (R, KT), 1)

    def sel_body(kt, carry):
        m_i, l_i, acc = carry
        start = pl.multiple_of(kt * KT, KT)
        kn = norm_k(ks_ref[pl.ds(start, KT), :])
        s = _dot_nt(qall, kn)
        expand = (e_m == e_j + kt * (KT // NSA_SEL_BLOCK)).astype(BF16)
        picked = _dot(sel_b, expand) > 0.5
        mask = jnp.concatenate([picked] * HG, axis=0) & (start + j_k <= t_k)
        s = jnp.where(mask, s, NEG)
        m_new = jnp.maximum(m_i, jnp.max(s, axis=-1, keepdims=True))
        alpha = jnp.exp(m_i - m_new)
        p = jnp.where(mask, jnp.exp(s - m_new), 0.0)
        l_new = alpha * l_i + jnp.sum(p, axis=-1, keepdims=True)
        acc = alpha * acc + _dot(p.astype(BF16), vs_ref[pl.ds(start, KT), :])
        return m_new, l_new, acc

    init = (jnp.full((R, 1), NEG, F32), jnp.zeros((R, 1), F32), jnp.zeros((R, DH), F32))
    _, l_s, acc_s = lax.fori_loop(0, q0 // KT + 1, sel_body, init)
    o_s = acc_s / jnp.where(l_s > 0, l_s, 1.0)

    span = NSA_WINDOW + TQ
    wstart = pl.multiple_of(jnp.maximum(q0 - NSA_WINDOW, 0), TQ)
    kwn = norm_k(kw_ref[pl.ds(wstart, span), :])
    s_w = _dot_nt(qall, kwn)
    kpos = wstart + lax.broadcasted_iota(jnp.int32, (R, span), 1)
    t_w = tok((R, span))
    p_w = _softmax_rows(s_w, (kpos <= t_w) & (kpos > t_w - NSA_WINDOW))
    o_w = _dot(p_w.astype(BF16), vw_ref[pl.ds(wstart, span), :])

    gts = jax.nn.sigmoid(pltpu.roll(gt_ref[...], (128 - HG * g) % 128, 1))
    stride = NSA_GROUPS * HG
    for h in range(HG):
        rows = slice(h * TQ, (h + 1) * TQ)
        o = (gts[:, h:h + 1] * o_c[rows] + gts[:, stride + h:stride + h + 1] * o_s[rows]
             + gts[:, 2 * stride + h:2 * stride + h + 1] * o_w[rows])
        o_ref[:, h * DH:(h + 1) * DH] = o.astype(BF16)


def nsa_attention(qkv, gates_raw, k_cmp, v_cmp, q_norm, k_norm, *, bsz, seq):
    t = qkv.shape[0]
    G, HG, DH, TQ = NSA_GROUPS, NSA_HPG, NSA_DH, NSA_QT
    nq = seq // TQ
    qw = HG * DH
    base = G * HG
    kv = lambda which: pl.BlockSpec((seq, DH), lambda b, g, i: (b, base + which * G + g))
    cmp_spec = pl.BlockSpec((None, None, k_cmp.shape[2], DH), lambda b, g, i: (b, g, 0, 0))
    vec = pl.BlockSpec((1, DH), lambda b, g, i: (0, 0))
    return pl.pallas_call(
        _nsa_attn_kernel,
        grid=(bsz, G, nq),
        in_specs=[
            pl.BlockSpec((TQ, qw), lambda b, g, i: (b * nq + i, g)),
            kv(2), kv(3), kv(4), kv(5), cmp_spec, cmp_spec,
            pl.BlockSpec((TQ, 128), lambda b, g, i: (b * nq + i, 0)),
            vec, vec,
        ],
        out_specs=pl.BlockSpec((TQ, qw), lambda b, g, i: (b * nq + i, g)),
        out_shape=jax.ShapeDtypeStruct((t, G * qw), BF16),
        compiler_params=_cparams(("parallel", "parallel", "arbitrary")),
        name="nsa_attention",
    )(qkv, qkv, qkv, qkv, qkv, k_cmp, v_cmp, gates_raw, q_norm.reshape(1, DH), k_norm.reshape(1, DH))


def nsa_layer(x, norm_w, w_in, q_norm, k_norm, cmp_pos, k_w1, k_w2, v_w1, v_w2, w_out, *, bsz, seq):
    G, DH = NSA_GROUPS, NSA_DH
    ngate = 3 * G * NSA_HPG
    nmain = w_in.shape[1] - ngate
    w_g = jnp.pad(w_in[:, nmain:], ((0, 0), (0, 128 - ngate)))
    qkv, gates_raw = rms_matmul(x, norm_w, w_in[:, :nmain].astype(BF16), w_g.astype(BF16))
    qd = G * NSA_HPG * DH
    kvd = G * DH

    def sub_blocks(cols):
        a = cols.reshape(bsz, seq // NSA_STRIDE, NSA_STRIDE, G, DH)
        return a.transpose(0, 3, 1, 2, 4).reshape(bsz, G, seq // NSA_STRIDE, NSA_STRIDE * DH)

    k_cmp, v_cmp = nsa_compress(sub_blocks(qkv[:, qd:qd + kvd]), sub_blocks(qkv[:, qd + kvd:qd + 2 * kvd]),
                                cmp_pos, k_w1, k_w2, v_w1, v_w2, k_norm)
    o = nsa_attention(qkv, gates_raw, k_cmp, v_cmp, q_norm, k_norm, bsz=bsz, seq=seq)
    return matmul_resid(o, w_out.astype(BF16), x)


def kernel(x, mix_norm, ffn_norm, ssm_w_in, ssm_conv_w, ssm_conv_b, ssm_dt_bias, ssm_a_log, ssm_d, ssm_norm, ssm_w_out, nsa_w_in, nsa_q_norm, nsa_k_norm, nsa_cmp_pos, nsa_cmp_k_w1, nsa_cmp_k_w2, nsa_cmp_v_w1, nsa_cmp_v_w2, nsa_w_out, gdn_w_in, gdn_conv_w, gdn_a_log, gdn_dt_bias, gdn_norm, gdn_w_out, sgu_w_in, sgu_norm, sgu_w_s, sgu_b_s, sgu_w_out, ffn_w_gu, ffn_conv_w, ffn_conv_b, ffn_w_down):
    bsz, seq, d = x.shape
    h = x.reshape(bsz * seq, d)
    depth = mix_norm.shape[0]
    for i in range(depth):
        m, j = i % 4, i // 4
        if m == 0:
            h = ssd_layer(h, mix_norm[i], ssm_w_in[j], ssm_conv_w[j], ssm_conv_b[j], ssm_dt_bias[j], ssm_a_log[j],
                          ssm_d[j], ssm_norm[j], ssm_w_out[j], bsz=bsz, seq=seq)
        elif m == 1:
            h = nsa_layer(h, mix_norm[i], nsa_w_in[j], nsa_q_norm[j], nsa_k_norm[j], nsa_cmp_pos[j], nsa_cmp_k_w1[j],
                          nsa_cmp_k_w2[j], nsa_cmp_v_w1[j], nsa_cmp_v_w2[j], nsa_w_out[j], bsz=bsz, seq=seq)
        elif m == 2:
            h = gdn_layer(h, mix_norm[i], gdn_w_in[j], gdn_conv_w[j], gdn_a_log[j], gdn_dt_bias[j], gdn_norm[j],
                          gdn_w_out[j], bsz=bsz, seq=seq)
        else:
            h = sgu_layer(h, mix_norm[i], sgu_w_in[j], sgu_norm[j], sgu_w_s[j], sgu_b_s[j], sgu_w_out[j])
        h = conv_ffn_layer(h, ffn_norm[i], ffn_w_gu[i], ffn_conv_w[i], ffn_conv_b[i], ffn_w_down[i], seq=seq)
    return h.reshape(bsz, seq, d)
```
